```python
import math, functools
import jax, jax.numpy as jnp
from jax import lax
import numpy as np

D_MODEL = 4096
BATCH = 8
SEQ = 2048
DEPTH = 2
DEC_BATCH = 32
DEC_SEQ = 64
PAST_LEN = 1024

CHUNK = 64
N_EVEN = (DEPTH + 1) // 2
N_ODD = DEPTH // 2
A_HEADS = 8
A_HEAD_DIM = 128
B_GROUPS = 8
B_GROUP_DIM = 256
B_CHUNK = 128
C_HEADS = 32
C_KV_HEADS = 4
C_HEAD_DIM = 128
IDX_HEADS = 32
IDX_DIM = 128
TOPK_MAX = 256
D_FF = 14336
N_EXPERTS = 8
TOP_K_EXPERTS = 2
PLE_DIM = 256
DENSE_QBLOCK = 128
SPARSE_QBLOCK = 16
EPS = 1e-6

A_QK = A_HEADS * 2 * A_HEAD_DIM
A_V = A_HEADS * 2 * A_HEAD_DIM
B_W = B_GROUPS * B_GROUP_DIM
EVEN_SPLITS = (A_QK, A_QK, A_V, B_W, B_W)
IN_E = sum(EVEN_SPLITS)
MIX_E = A_V + B_W
ODD_SPLITS = (C_HEADS * C_HEAD_DIM, C_KV_HEADS * C_HEAD_DIM, C_KV_HEADS * C_HEAD_DIM,
              IDX_HEADS * IDX_DIM, IDX_DIM, IDX_HEADS)
IN_O = sum(ODD_SPLITS)
MIX_O = C_HEADS * C_HEAD_DIM

kernel_name = 'hybrid_stream_diffattn_gmlp_dsa_step'


def _rmsnorm(x, g):
    xf = x.astype(jnp.float32)
    y = xf * lax.rsqrt(jnp.mean(xf * xf, axis=-1, keepdims=True) + EPS)
    return (y * g.astype(jnp.float32)).astype(x.dtype)


def _split(z, sizes):
    idx = [int(s) for s in np.cumsum(sizes)[:-1]]
    return jnp.split(z, idx, axis=-1)


def _query_blocked(fn, blk, *args):
    t = args[0].shape[1]
    blk = min(blk, t)
    nb = -(-t // blk)
    pad = nb * blk - t

    def to_blocks(a):
        a = jnp.pad(a, [(0, 0), (0, pad)] + [(0, 0)] * (a.ndim - 2), mode='edge')
        return jnp.moveaxis(a.reshape(a.shape[0], nb, blk, *a.shape[2:]), 1, 0)

    out = lax.map(lambda xs: fn(*xs), tuple(to_blocks(a) for a in args))
    out = jnp.moveaxis(out, 0, 1)
    return out.reshape(out.shape[0], nb * blk, *out.shape[3:])[:, :t]


def _diff_attn_block(k, v, k_pos, lam, qb, qpb):
    s = jnp.einsum('bthcd,bshcd->bhcts', qb, k).astype(jnp.float32) * (A_HEAD_DIM ** -0.5)
    visible = (k_pos[None, :] // CHUNK) <= (qpb[0][:, None] // CHUNK)
    pr = jax.nn.softmax(jnp.where(visible, s, -jnp.inf), axis=-1)
    a = (pr[:, :, 0] - lam * pr[:, :, 1]).astype(v.dtype)
    return jnp.einsum('bhts,bshe->bthe', a, v)


def _chunk_mlp(u, v, ws, bias):
    bsz, t, g, dg = v.shape
    nc = -(-t // B_CHUNK)
    pad = nc * B_CHUNK - t
    vp = jnp.pad(v, ((0, 0), (0, pad), (0, 0), (0, 0))).reshape(bsz, nc, B_CHUNK, g, dg)
    causal = jnp.tril(jnp.ones((B_CHUNK, B_CHUNK), dtype=ws.dtype))
    s = jnp.einsum('gts,bcsgd->bctgd', ws * causal, vp) + bias.T[None, None, :, :, None]
    return u * s.reshape(bsz, nc * B_CHUNK, g, dg)[:, :t]


def _dsa_block(k, v, ki, k_pos, topk, qb, qib, wb, qpb):
    bsz, t, nh, dh = qb.shape
    dots = jnp.einsum('bthd,bsd->bths', qib.astype(jnp.float32), ki.astype(jnp.float32)) * (IDX_DIM ** -0.5)
    score = jnp.einsum('bths,bth->bts', jax.nn.relu(dots), wb.astype(jnp.float32) * (IDX_HEADS ** -0.5))
    admissible = (k_pos[None, None, :] // CHUNK) <= (qpb[:, :, None] // CHUNK)
    score = jnp.where(admissible, score, -jnp.inf)
    vals, idx = lax.top_k(score, topk)
    valid = jnp.isfinite(vals)
    gather = jax.vmap(lambda a, i: a[i])
    kg = gather(k, idx)
    vg = gather(v, idx)
    qg = qb.reshape(bsz, t, C_KV_HEADS, nh // C_KV_HEADS, dh)
    s = jnp.einsum('btgrd,btkgd->btgrk', qg, kg).astype(jnp.float32) * (dh ** -0.5)
    pr = jax.nn.softmax(jnp.where(valid[:, :, None, None, :], s, -jnp.inf), axis=-1).astype(vg.dtype)
    o = jnp.einsum('btgrk,btkgd->btgrd', pr, vg)
    return o.reshape(bsz, t, nh * dh)


def _swiglu(x, w1, w3, w2):
    return (jax.nn.silu(x @ w1) * (x @ w3)) @ w2


def _moe(x, router, w1, w3, w2):
    logits = (x @ router).astype(jnp.float32)
    top_vals, top_idx = lax.top_k(logits, TOP_K_EXPERTS)
    top_w = jax.nn.softmax(top_vals, axis=-1)
    gate = jnp.sum(jax.nn.one_hot(top_idx, N_EXPERTS, dtype=jnp.float32) * top_w[..., None], axis=-2).astype(x.dtype)
    out = jnp.zeros_like(x)
    for e in range(N_EXPERTS):
        out = out + gate[..., e:e + 1] * _swiglu(x, w1[e], w3[e], w2[e])
    return out


def setup_inputs(seed: int = 0) -> dict:
    key = jax.random.key(seed)
    ks = iter(jax.random.split(key, 48))

    def nrm(shape, scale=1.0):
        return jax.random.normal(next(ks), shape, jnp.float32) * scale

    def gain(shape):
        return 1.0 + 0.02 * jax.random.normal(next(ks), shape, jnp.float32)

    E, O, D = N_EVEN, N_ODD, D_MODEL
    return {
        'x_prompt': nrm((BATCH, SEQ, D)),
        'x_sample': nrm((DEC_BATCH, DEC_SEQ, D)),
        'cache_a_k': nrm((E, DEC_BATCH, PAST_LEN, A_HEADS, 2, A_HEAD_DIM)),
        'cache_a_v': nrm((E, DEC_BATCH, PAST_LEN, A_HEADS, 2 * A_HEAD_DIM)),
        'cache_c_k': nrm((O, DEC_BATCH, PAST_LEN, C_KV_HEADS, C_HEAD_DIM)),
        'cache_c_v': nrm((O, DEC_BATCH, PAST_LEN, C_KV_HEADS, C_HEAD_DIM)),
        'cache_c_kidx': nrm((O, DEC_BATCH, PAST_LEN, IDX_DIM)),
        'p_prompt': nrm((DEPTH, BATCH, SEQ, PLE_DIM)),
        'p_sample': nrm((DEPTH, DEC_BATCH, DEC_SEQ, PLE_DIM)),
        'ln_mix_e': gain((E, D)),
        'w_in_e': nrm((E, D, IN_E), D ** -0.5),
        'a_qnorm': gain((E, A_HEAD_DIM)),
        'a_knorm': gain((E, A_HEAD_DIM)),
        'a_lam': nrm((E, 4, A_HEAD_DIM), 0.1),
        'a_subln': gain((E, 2 * A_HEAD_DIM)),
        'b_vnorm': gain((E, B_GROUPS, B_GROUP_DIM)),
        'b_ws': nrm((E, B_GROUPS, B_CHUNK, B_CHUNK), B_CHUNK ** -0.5),
        'b_bias': 1.0 + nrm((E, B_GROUPS, B_CHUNK), 0.1),
        'w_out_e': nrm((E, MIX_E, D), MIX_E ** -0.5),
        'ln_ffn_e': gain((E, D)),
        'ffn_w1': nrm((E, D, D_FF), D ** -0.5),
        'ffn_w3': nrm((E, D, D_FF), D ** -0.5),
        'ffn_w2': nrm((E, D_FF, D), D_FF ** -0.5),
        'ln_mix_o': gain((O, D)),
        'w_in_o': nrm((O, D, IN_O), D ** -0.5),
        'c_qnorm': gain((O, C_HEAD_DIM)),
        'c_knorm': gain((O, C_HEAD_DIM)),
        'c_kidx_norm': gain((O, IDX_DIM)),
        'w_out_o': nrm((O, MIX_O, D), MIX_O ** -0.5),
        'ln_ffn_o': gain((O, D)),
        'moe_router': nrm((O, D, N_EXPERTS), D ** -0.5),
        'moe_w1': nrm((O, N_EXPERTS, D, D_FF), D ** -0.5),
        'moe_w3': nrm((O, N_EXPERTS, D, D_FF), D ** -0.5),
        'moe_w2': nrm((O, N_EXPERTS, D_FF, D), D_FF ** -0.5),
        'ple_proj': nrm((DEPTH, PLE_DIM, D), PLE_DIM ** -0.5),
        'ple_gate': nrm((DEPTH, D, D), D ** -0.5),
        'ln_ple': gain((DEPTH, D)),
    }


def reference(x_prompt, x_sample, cache_a_k, cache_a_v, cache_c_k, cache_c_v, cache_c_kidx,
              p_prompt, p_sample, ln_mix_e, w_in_e, a_qnorm, a_knorm, a_lam, a_subln, b_vnorm,
              b_ws, b_bias, w_out_e, ln_ffn_e, ffn_w1, ffn_w3, ffn_w2, ln_mix_o, w_in_o, c_qnorm,
              c_knorm, c_kidx_norm, w_out_o, ln_ffn_o, moe_router, moe_w1, moe_w3, moe_w2,
              ple_proj, ple_gate, ln_ple):

    def trunk(x, p, past):
        bsz, t_new, _ = x.shape
        p0 = 0 if past is None else past[0].shape[2]
        q_pos = (p0 + jnp.arange(t_new, dtype=jnp.int32))[None, :]
        k_pos = jnp.arange(p0 + t_new, dtype=jnp.int32)

        def with_past(slot, j, new):
            return new if past is None else jnp.concatenate([past[slot][j], new], axis=1)

        h = x
        a_k, a_v, b_v, c_k, c_v, c_ki = [], [], [], [], [], []
        for i in range(DEPTH):
            j = i // 2
            if i % 2 == 0:
                hn = _rmsnorm(h, ln_mix_e[j])
                qa, ka, va, ub, vb = _split(hn @ w_in_e[j], EVEN_SPLITS)
                qa = _rmsnorm(qa.reshape(bsz, t_new, A_HEADS, 2, A_HEAD_DIM), a_qnorm[j])
                ka = _rmsnorm(ka.reshape(bsz, t_new, A_HEADS, 2, A_HEAD_DIM), a_knorm[j])
                va = va.reshape(bsz, t_new, A_HEADS, 2 * A_HEAD_DIM)
                a_k.append(ka)
                a_v.append(va)
                lam_init = 0.8 - 0.6 * math.exp(-0.3 * i)
                lq = a_lam[j].astype(jnp.float32)
                lam = jnp.exp(jnp.sum(lq[0] * lq[1])) - jnp.exp(jnp.sum(lq[2] * lq[3])) + lam_init
                oa = _query_blocked(
                    functools.partial(_diff_attn_block, with_past(0, j, ka), with_past(1, j, va), k_pos, lam),
                    DENSE_QBLOCK, qa, q_pos)
                oa = (_rmsnorm(oa, a_subln[j]) * (1.0 - lam_init)).reshape(bsz, t_new, A_V)
                ub = jax.nn.gelu(ub.reshape(bsz, t_new, B_GROUPS, B_GROUP_DIM))
                vb = _rmsnorm(jax.nn.gelu(vb.reshape(bsz, t_new, B_GROUPS, B_GROUP_DIM)), b_vnorm[j])
                b_v.append(vb)
                ob = _chunk_mlp(ub, vb, b_ws[j], b_bias[j]).reshape(bsz, t_new, B_W)
                h = h + jnp.concatenate([oa, ob], axis=-1) @ w_out_e[j]
                h = h + _swiglu(_rmsnorm(h, ln_ffn_e[j]), ffn_w1[j], ffn_w3[j], ffn_w2[j])
            else:
                hn = _rmsnorm(h, ln_mix_o[j])
                qc, kc, vc, qi, ki, wi = _split(hn @ w_in_o[j], ODD_SPLITS)
                qc = _rmsnorm(qc.reshape(bsz, t_new, C_HEADS, C_HEAD_DIM), c_qnorm[j])
                kc = _rmsnorm(kc.reshape(bsz, t_new, C_KV_HEADS, C_HEAD_DIM), c_knorm[j])
                vc = vc.reshape(bsz, t_new, C_KV_HEADS, C_HEAD_DIM)
                qi = qi.reshape(bsz, t_new, IDX_HEADS, IDX_DIM)
                ki = _rmsnorm(ki, c_kidx_norm[j])
                c_k.append(kc)
                c_v.append(vc)
                c_ki.append(ki)
                topk = min(TOPK_MAX, k_pos.shape[0] // 4)
                oc = _query_blocked(
                    functools.partial(_dsa_block, with_past(2, j, kc), with_past(3, j, vc),
                                      with_past(4, j, ki), k_pos, topk),
                    SPARSE_QBLOCK, qc, qi, wi, q_pos)
                h = h + oc @ w_out_o[j]
                h = h + _moe(_rmsnorm(h, ln_ffn_o[j]), moe_router[j], moe_w1[j], moe_w3[j], moe_w2[j])
            h = h + (p[i] @ ple_proj[i]) * jax.nn.sigmoid(_rmsnorm(h, ln_ple[i]) @ ple_gate[i])
        return (h, jnp.stack(a_k), jnp.stack(a_v), jnp.stack(b_v),
                jnp.stack(c_k), jnp.stack(c_v), jnp.stack(c_ki))

    y_prompt, ak_p, av_p, _, ck_p, cv_p, cki_p = trunk(x_prompt, p_prompt, None)
    y_sample, ak_s, av_s, bv_s, ck_s, cv_s, cki_s = trunk(
        x_sample, p_sample, (cache_a_k, cache_a_v, cache_c_k, cache_c_v, cache_c_kidx))
    return (y_prompt, y_sample, ak_p, av_p, ck_p, cv_p, cki_p, ak_s, av_s, bv_s, ck_s, cv_s, cki_s)
```

```python
import functools
import math

import numpy as np
import jax
import jax.numpy as jnp
from jax import lax
from jax.experimental import pallas as pl
from jax.experimental.pallas import tpu as pltpu

_CDT = jnp.bfloat16
_F32 = jnp.float32
_EPS = 1e-6
_LANES = 128
_VMEM_LIMIT = 56 * 1024 * 1024

_CHUNK = 64
_A_HEADS = 8
_A_HEAD_DIM = 128
_B_GROUPS = 8
_B_GROUP_DIM = 256
_B_CHUNK = 128
_C_HEADS = 32
_C_KV_HEADS = 4
_C_HEAD_DIM = 128
_IDX_HEADS = 32
_IDX_DIM = 128
_TOPK_MAX = 256
_N_EXPERTS = 8
_MOE_TM = 512

_NT = (((1,), (1,)), ((), ()))


def _tile(n, pref, align=8):
    if n <= pref:
        return n
    for t in range(pref - pref % align, 0, -align):
        if n % t == 0:
            return t
    raise ValueError(f"no tile for {n} <= {pref}")


def _cp(*sem, vmem=_VMEM_LIMIT):
    return pltpu.CompilerParams(dimension_semantics=sem, vmem_limit_bytes=vmem)


def _rms(x, g):
    return x * lax.rsqrt(jnp.mean(x * x, axis=-1, keepdims=True) + _EPS) * g


def _rmsnorm_kernel(x_ref, g_ref, o_ref):
    o_ref[...] = _rms(x_ref[...], g_ref[...]).astype(o_ref.dtype)


def _rmsnorm(x, g):
    m, d = x.shape
    tm = _tile(m, 256)
    return pl.pallas_call(
        _rmsnorm_kernel,
        grid=(m // tm,),
        in_specs=[pl.BlockSpec((tm, d), lambda i: (i, 0)), pl.BlockSpec((1, d), lambda i: (0, 0))],
        out_specs=pl.BlockSpec((tm, d), lambda i: (i, 0)),
        out_shape=jax.ShapeDtypeStruct((m, d), _CDT),
        compiler_params=_cp("parallel"),
        name="rmsnorm",
    )(x, g.reshape(1, d))


def _gnorm_kernel(z_ref, g_ref, *o_refs, groups):
    g = g_ref[...]
    for i in range(groups):
        sl = slice(i * _LANES, (i + 1) * _LANES)
        y = _rms(z_ref[:, sl], g)
        for o in o_refs:
            o[:, sl] = y.astype(o.dtype)


def _gnorm(z, col_block, width, g, out_dtypes):
    m = z.shape[0]
    tm = _tile(m, 512)
    outs = pl.pallas_call(
        functools.partial(_gnorm_kernel, groups=width // _LANES),
        grid=(m // tm,),
        in_specs=[pl.BlockSpec((tm, width), lambda i: (i, col_block)),
                  pl.BlockSpec((1, _LANES), lambda i: (0, 0))],
        out_specs=[pl.BlockSpec((tm, width), lambda i: (i, 0)) for _ in out_dtypes],
        out_shape=[jax.ShapeDtypeStruct((m, width), dt) for dt in out_dtypes],
        compiler_params=_cp("parallel"),
        name="head_rmsnorm",
    )(z, g.reshape(1, _LANES))
    return outs


def _mm_kernel(a_ref, b_ref, o_ref):
    o_ref[...] = jnp.dot(a_ref[...], b_ref[...], preferred_element_type=_F32).astype(o_ref.dtype)


def _mm_res_kernel(a_ref, b_ref, r_ref, o_ref):
    o_ref[...] = r_ref[...] + jnp.dot(a_ref[...], b_ref[...], preferred_element_type=_F32)


def _mm(a, b, res=None, tm=1024, tn=1024):
    m, k = a.shape
    n = b.shape[1]
    tm, tn = _tile(m, tm), _tile(n, tn, _LANES)
    in_specs = [pl.BlockSpec((tm, k), lambda j, i: (i, 0)), pl.BlockSpec((k, tn), lambda j, i: (0, j))]
    args = [a, b]
    if res is not None:
        in_specs.append(pl.BlockSpec((tm, tn), lambda j, i: (i, j)))
        args.append(res)
    return pl.pallas_call(
        _mm_kernel if res is None else _mm_res_kernel,
        grid=(n // tn, m // tm),
        in_specs=in_specs,
        out_specs=pl.BlockSpec((tm, tn), lambda j, i: (i, j)),
        out_shape=jax.ShapeDtypeStruct((m, n), _F32),
        compiler_params=_cp("parallel", "parallel"),
        name="matmul",
    )(*args)


def _swiglu_kernel(a_ref, w1_ref, w3_ref, o_ref):
    a = a_ref[...]
    g = jnp.dot(a, w1_ref[...], preferred_element_type=_F32)
    u = jnp.dot(a, w3_ref[...], preferred_element_type=_F32)
    o_ref[...] = (g * jax.nn.sigmoid(g) * u).astype(o_ref.dtype)


def _mm_swiglu(a, w1, w3, tm=1024, tn=512):
    m, k = a.shape
    n = w1.shape[1]
    tm, tn = _tile(m, tm), _tile(n, tn, _LANES)
    wspec = pl.BlockSpec((k, tn), lambda j, i: (0, j))
    return pl.pallas_call(
        _swiglu_kernel,
        grid=(n // tn, m // tm),
        in_specs=[pl.BlockSpec((tm, k), lambda j, i: (i, 0)), wspec, wspec],
        out_specs=pl.BlockSpec((tm, tn), lambda j, i: (i, j)),
        out_shape=jax.ShapeDtypeStruct((m, n), _CDT),
        compiler_params=_cp("parallel", "parallel"),
        name="swiglu_up",
    )(a, w1, w3)


def _mm_kres_kernel(a_ref, b_ref, r_ref, o_ref):
    d = jnp.dot(a_ref[...], b_ref[...], preferred_element_type=_F32)

    @pl.when(pl.program_id(2) == 0)
    def _():
        o_ref[...] = r_ref[...] + d

    @pl.when(pl.program_id(2) != 0)
    def _():
        o_ref[...] += d


def _mm_kres(a, b, res, tm=1024, tn=1024, tk=2048):
    m, k = a.shape
    n = b.shape[1]
    tm, tn, tk = _tile(m, tm), _tile(n, tn, _LANES), _tile(k, tk, _LANES)
    return pl.pallas_call(
        _mm_kres_kernel,
        grid=(n // tn, m // tm, k // tk),
        in_specs=[pl.BlockSpec((tm, tk), lambda j, i, kk: (i, kk)),
                  pl.BlockSpec((tk, tn), lambda j, i, kk: (kk, j)),
                  pl.BlockSpec((tm, tn), lambda j, i, kk: (i, j))],
        out_specs=pl.BlockSpec((tm, tn), lambda j, i, kk: (i, j)),
        out_shape=jax.ShapeDtypeStruct((m, n), _F32),
        compiler_params=_cp("parallel", "parallel", "arbitrary"),
        name="matmul_ksplit",
    )(a, b, res)


def _ple_kernel(hn_ref, gw_ref, p_ref, pw_ref, h_ref, o_ref):
    gate = jnp.dot(hn_ref[...], gw_ref[...], preferred_element_type=_F32)
    emb = jnp.dot(p_ref[...], pw_ref[...], preferred_element_type=_F32)
    o_ref[...] = h_ref[...] + emb * jax.nn.sigmoid(gate)


def _ple(hn, gate_w, p, proj_w, h, tm=1024, tn=512):
    m, k = hn.shape
    n = gate_w.shape[1]
    pd = p.shape[1]
    tm, tn = _tile(m, tm), _tile(n, tn, _LANES)
    return pl.pallas_call(
        _ple_kernel,
        grid=(n // tn, m // tm),
        in_specs=[pl.BlockSpec((tm, k), lambda j, i: (i, 0)),
                  pl.BlockSpec((k, tn), lambda j, i: (0, j)),
                  pl.BlockSpec((tm, pd), lambda j, i: (i, 0)),
                  pl.BlockSpec((pd, tn), lambda j, i: (0, j)),
                  pl.BlockSpec((tm, tn), lambda j, i: (i, j))],
        out_specs=pl.BlockSpec((tm, tn), lambda j, i: (i, j)),
        out_shape=jax.ShapeDtypeStruct((m, n), _F32),
        compiler_params=_cp("parallel", "parallel"),
        name="ple_gate",
    )(hn, gate_w, p, proj_w, h)


def _diff_attn_kernel(q_ref, k_ref, v_ref, lam_ref, sub_ref, o_ref, *, tq, lp, l_valid, p0, lam_init):
    i = pl.program_id(2)
    lq = lam_ref[...]
    lam = (jnp.exp(jnp.sum(lq[0:1] * lq[1:2], axis=-1, keepdims=True))
           - jnp.exp(jnp.sum(lq[2:3] * lq[3:4], axis=-1, keepdims=True)) + lam_init)
    qpos = p0 + i * tq + lax.broadcasted_iota(jnp.int32, (tq, lp), 0)
    kpos = lax.broadcasted_iota(jnp.int32, (tq, lp), 1)
    visible = ((kpos >> 6) <= (qpos >> 6)) & (kpos < l_valid)
    scale = _A_HEAD_DIM ** -0.5
    probs = []
    for c in range(2):
        sl = slice(c * _A_HEAD_DIM, (c + 1) * _A_HEAD_DIM)
        s = lax.dot_general(q_ref[:, sl], k_ref[:, sl], _NT, preferred_element_type=_F32) * scale
        s = jnp.where(visible, s, -jnp.inf)
        e = jnp.exp(s - jnp.max(s, axis=-1, keepdims=True))
        probs.append(e / jnp.sum(e, axis=-1, keepdims=True))
    a = (probs[0] - lam * probs[1]).astype(_CDT)
    o = jnp.dot(a, v_ref[...], preferred_element_type=_F32)
    o_ref[...] = (_rms(o, sub_ref[...]) * (1.0 - lam_init)).astype(o_ref.dtype)


def _diff_attn(q, k, v, lam, subln, *, bsz, t, lp, l_valid, p0, q_row0, lam_init):
    hw = 2 * _A_HEAD_DIM
    tq = _tile(t, 256)
    nq = t // tq
    qb0 = q_row0 // tq
    kern = functools.partial(_diff_attn_kernel, tq=tq, lp=lp, l_valid=l_valid, p0=p0, lam_init=lam_init)
    return pl.pallas_call(
        kern,
        grid=(bsz, _A_HEADS, nq),
        in_specs=[pl.BlockSpec((tq, hw), lambda b, h, i: (qb0 + b * nq + i, h)),
                  pl.BlockSpec((lp, hw), lambda b, h, i: (b, h)),
                  pl.BlockSpec((lp, hw), lambda b, h, i: (b, h)),
                  pl.BlockSpec((4, _A_HEAD_DIM), lambda b, h, i: (0, 0)),
                  pl.BlockSpec((1, hw), lambda b, h, i: (0, 0))],
        out_specs=pl.BlockSpec((tq, hw), lambda b, h, i: (b * nq + i, h)),
        out_shape=jax.ShapeDtypeStruct((bsz * t, _A_HEADS * hw), _CDT),
        compiler_params=_cp("parallel", "parallel", "parallel"),
        name="diff_attention",
    )(q, k, v, lam, subln.reshape(1, hw))


def _chunk_mlp_kernel(u_ref, v_ref, g_ref, ws_ref, b_ref, o_ref, vo_ref, *, tc):
    row = lax.broadcasted_iota(jnp.int32, (tc, tc), 0)
    col = lax.broadcasted_iota(jnp.int32, (tc, tc), 1)
    for g in range(_B_GROUPS):
        sl = slice(g * _B_GROUP_DIM, (g + 1) * _B_GROUP_DIM)
        u = jax.nn.gelu(u_ref[:, sl])
        v = _rms(jax.nn.gelu(v_ref[:, sl]), g_ref[g])
        vo_ref[:, sl] = v
        w = jnp.where(row >= col, ws_ref[g], 0.0).astype(_CDT)
        s = jnp.dot(w, v.astype(_CDT), preferred_element_type=_F32) + b_ref[g]
        o_ref[:, sl] = (u * s).astype(o_ref.dtype)


def _chunk_mlp(z, u_col, v_col, vnorm, ws, bias, *, rows, row0, tc):
    bw = _B_GROUPS * _B_GROUP_DIM
    rb0 = row0 // tc
    return pl.pallas_call(
        functools.partial(_chunk_mlp_kernel, tc=tc),
        grid=(rows // tc,),
        in_specs=[pl.BlockSpec((tc, bw), lambda i: (rb0 + i, u_col)),
                  pl.BlockSpec((tc, bw), lambda i: (rb0 + i, v_col)),
                  pl.BlockSpec((_B_GROUPS, 1, _B_GROUP_DIM), lambda i: (0, 0, 0)),
                  pl.BlockSpec((_B_GROUPS, tc, tc), lambda i: (0, 0, 0)),
                  pl.BlockSpec((_B_GROUPS, tc, 1), lambda i: (0, 0, 0))],
        out_specs=[pl.BlockSpec((tc, bw), lambda i: (i, 0)), pl.BlockSpec((tc, bw), lambda i: (i, 0))],
        out_shape=[jax.ShapeDtypeStruct((rows, bw), _CDT), jax.ShapeDtypeStruct((rows, bw), _F32)],
        compiler_params=_cp("parallel"),
        name="chunk_mlp",
    )(z, z, vnorm.reshape(_B_GROUPS, 1, _B_GROUP_DIM), ws[:, :tc, :tc], bias[:, :tc].reshape(_B_GROUPS, tc, 1))


def _sortable_key(x):
    bits = lax.bitcast_convert_type(x, jnp.int32)
    return jnp.where(bits >= 0, bits, bits ^ jnp.int32(0x7FFFFFFF))


_KEY_NEG_INF = int(np.array(-np.inf, np.float32).view(np.int32)) ^ 0x7FFFFFFF
_KEY_NEG_INF = _KEY_NEG_INF - (1 << 32) if _KEY_NEG_INF >= (1 << 31) else _KEY_NEG_INF
_INT_MIN = -(1 << 31)


def _dsa_index_kernel(qi_ref, wi_ref, ki_ref, tri_ref, mask_ref, acc_ref, *, tq, lp, l_valid, p0, topk, heads_per_step):
    i = pl.program_id(1)
    s = pl.program_id(2)

    @pl.when(s == 0)
    def _():
        acc_ref[...] = jnp.zeros_like(acc_ref)

    ki = ki_ref[...]
    w = wi_ref[...]
    lane = lax.broadcasted_iota(jnp.int32, w.shape, 1)
    acc = acc_ref[...]
    for r in range(heads_per_step):
        qh = qi_ref[:, r * _IDX_DIM:(r + 1) * _IDX_DIM].astype(_CDT)
        d = lax.dot_general(qh, ki, _NT, preferred_element_type=_F32) * (_IDX_DIM ** -0.5)
        wcol = jnp.sum(jnp.where(lane == s * heads_per_step + r, w, 0.0), axis=-1, keepdims=True)
        acc = acc + jnp.maximum(d, 0.0) * (wcol * (_IDX_HEADS ** -0.5))
    acc_ref[...] = acc

    @pl.when(s == pl.num_programs(2) - 1)
    def _():
        qpos = p0 + i * tq + lax.broadcasted_iota(jnp.int32, (tq, lp), 0)
        kpos = lax.broadcasted_iota(jnp.int32, (tq, lp), 1)
        admissible = ((kpos >> 6) <= (qpos >> 6)) & (kpos < l_valid)
        score = jnp.where(acc == 0.0, 0.0, acc)
        key = _sortable_key(jnp.where(admissible, score, -jnp.inf))

        def bit_step(b, t):
            cand = t | (jnp.int32(1) << (31 - b))
            cnt = jnp.sum(jnp.where(key >= (cand ^ jnp.int32(_INT_MIN)), 1.0, 0.0), axis=-1, keepdims=True)
            return jnp.where(cnt >= topk, cand, t)

        thr = lax.fori_loop(0, 32, bit_step, jnp.zeros((tq, 1), jnp.int32)) ^ jnp.int32(_INT_MIN)
        gt = key > thr
        eq = key == thr
        n_gt = jnp.sum(jnp.where(gt, 1.0, 0.0), axis=-1, keepdims=True)
        rank = jnp.dot(jnp.where(eq, 1.0, 0.0).astype(_CDT), tri_ref[...], preferred_element_type=_F32)
        take_tie = jnp.where(rank < (topk - n_gt), 0.0, -jnp.inf)
        mask = jnp.where(gt, 0.0, jnp.where(eq, take_tie, -jnp.inf))
        mask_ref[...] = jnp.where(key > jnp.int32(_KEY_NEG_INF), mask, -jnp.inf)


def _dsa_index(zq, qi_col0, zw, wi_col, ki, tri, *, bsz, t, lp, l_valid, p0, q_row0, topk):
    tq = _tile(t, 128)
    nq = t // tq
    qb0 = q_row0 // tq
    hps = 8
    kern = functools.partial(_dsa_index_kernel, tq=tq, lp=lp, l_valid=l_valid, p0=p0, topk=topk, heads_per_step=hps)
    return pl.pallas_call(
        kern,
        grid=(bsz, nq, _IDX_HEADS // hps),
        in_specs=[pl.BlockSpec((tq, hps * _IDX_DIM), lambda b, i, s: (qb0 + b * nq + i, qi_col0 + s)),
                  pl.BlockSpec((tq, _LANES), lambda b, i, s: (qb0 + b * nq + i, wi_col)),
                  pl.BlockSpec((lp, _IDX_DIM), lambda b, i, s: (b, 0)),
                  pl.BlockSpec((lp, lp), lambda b, i, s: (0, 0))],
        out_specs=pl.BlockSpec((tq, lp), lambda b, i, s: (b * nq + i, 0)),
        out_shape=jax.ShapeDtypeStruct((bsz * t, lp), _F32),
        scratch_shapes=[pltpu.VMEM((tq, lp), _F32)],
        compiler_params=_cp("parallel", "parallel", "arbitrary"),
        name="dsa_index_select",
    )(zq, zw, ki, tri)


def _dsa_attn_kernel(q_ref, k_ref, v_ref, mask_ref, o_ref, *, heads_per_kv):
    k = k_ref[...]
    v = v_ref[...]
    mask = mask_ref[...]
    scale = _C_HEAD_DIM ** -0.5
    for r in range(heads_per_kv):
        sl = slice(r * _C_HEAD_DIM, (r + 1) * _C_HEAD_DIM)
        s = lax.dot_general(q_ref[:, sl], k, _NT, preferred_element_type=_F32) * scale + mask
        e = jnp.exp(s - jnp.max(s, axis=-1, keepdims=True))
        p = (e / jnp.sum(e, axis=-1, keepdims=True)).astype(_CDT)
        o_ref[:, sl] = jnp.dot(p, v, preferred_element_type=_F32).astype(o_ref.dtype)


def _dsa_attn(q, k, v, mask, *, bsz, t, lp, q_row0):
    tq = _tile(t, 128)
    nq = t // tq
    qb0 = q_row0 // tq
    hpk = _C_HEADS // _C_KV_HEADS
    gw = hpk * _C_HEAD_DIM
    return pl.pallas_call(
        functools.partial(_dsa_attn_kernel, heads_per_kv=hpk),
        grid=(bsz, nq, _C_KV_HEADS),
        in_specs=[pl.BlockSpec((tq, gw), lambda b, i, g: (qb0 + b * nq + i, g)),
                  pl.BlockSpec((lp, _C_HEAD_DIM), lambda b, i, g: (b, g)),
                  pl.BlockSpec((lp, _C_HEAD_DIM), lambda b, i, g: (b, g)),
                  pl.BlockSpec((tq, lp), lambda b, i, g: (b * nq + i, 0))],
        out_specs=pl.BlockSpec((tq, gw), lambda b, i, g: (b * nq + i, g)),
        out_shape=jax.ShapeDtypeStruct((bsz * t, _C_HEADS * _C_HEAD_DIM), _CDT),
        compiler_params=_cp("parallel", "parallel", "parallel"),
        name="dsa_attention",
    )(q, k, v, mask)


def _router_kernel(h_ref, g_ref, r_ref, o_ref):
    xn = _rms(h_ref[...], g_ref[...]).astype(_CDT)
    logits = jnp.dot(xn, r_ref[...], preferred_element_type=_F32)
    lane = lax.broadcasted_iota(jnp.int32, logits.shape, 1).astype(_F32)
    logits = jnp.where(lane < _N_EXPERTS, logits, -jnp.inf)
    m1 = jnp.max(logits, axis=-1, keepdims=True)
    i1 = jnp.min(jnp.where(logits == m1, lane, float(_LANES)), axis=-1, keepdims=True)
    rest = jnp.where(lane == i1, -jnp.inf, logits)
    m2 = jnp.max(rest, axis=-1, keepdims=True)
    i2 = jnp.min(jnp.where(rest == m2, lane, float(_LANES)), axis=-1, keepdims=True)
    e2 = jnp.exp(m2 - m1)
    w1 = 1.0 / (1.0 + e2)
    w2 = e2 / (1.0 + e2)
    o_ref[...] = jnp.where(lane == 0, i1, jnp.where(lane == 1, i2, jnp.where(lane == 2, w1, jnp.where(lane == 3, w2, 0.0))))


def _router(h, g, router_w):
    m, d = h.shape
    tm = _tile(m, 256)
    return pl.pallas_call(
        _router_kernel,
        grid=(m // tm,),
        in_specs=[pl.BlockSpec((tm, d), lambda i: (i, 0)), pl.BlockSpec((1, d), lambda i: (0, 0)),
                  pl.BlockSpec((d, _LANES), lambda i: (0, 0))],
        out_specs=pl.BlockSpec((tm, _LANES), lambda i: (i, 0)),
        out_shape=jax.ShapeDtypeStruct((m, _LANES), _F32),
        compiler_params=_cp("parallel"),
        name="moe_router",
    )(h, g.reshape(1, d), router_w)


def _row_copy(src_hbm, row, dst, sem):
    return pltpu.make_async_copy(src_hbm.at[pl.ds(row, 1)], dst, sem)


def _gather_norm_kernel(src_ref, h_hbm, g_ref, o_ref, buf, sem, *, tg):
    base = pl.program_id(0) * tg

    def start(r, c):
        _row_copy(h_hbm, src_ref[base + r], buf.at[pl.ds(r, 1)], sem).start()
        return c

    lax.fori_loop(0, tg, start, 0)

    def wait(r, c):
        _row_copy(h_hbm, 0, buf.at[pl.ds(r, 1)], sem).wait()
        return c

    lax.fori_loop(0, tg, wait, 0)
    o_ref[...] = _rms(buf[...], g_ref[...]).astype(o_ref.dtype)


def _gather_norm(src, h, g, tg=256):
    p = src.shape[0]
    d = h.shape[1]
    tg = _tile(p, tg)
    return pl.pallas_call(
        functools.partial(_gather_norm_kernel, tg=tg),
        grid_spec=pltpu.PrefetchScalarGridSpec(
            num_scalar_prefetch=1,
            grid=(p // tg,),
            in_specs=[pl.BlockSpec(memory_space=pl.ANY), pl.BlockSpec((1, d), lambda i, s: (0, 0))],
            out_specs=pl.BlockSpec((tg, d), lambda i, s: (i, 0)),
            scratch_shapes=[pltpu.VMEM((tg, d), _F32), pltpu.SemaphoreType.DMA]),
        out_shape=jax.ShapeDtypeStruct((p, d), _CDT),
        compiler_params=_cp("arbitrary"),
        name="moe_gather_norm",
    )(src, h, g.reshape(1, d))


def _moe_up_kernel(tid_ref, te_ref, nv_ref, a_ref, w1_ref, w3_ref, o_ref):
    @pl.when(pl.program_id(1) < nv_ref[0])
    def _():
        a = a_ref[...]
        g = jnp.dot(a, w1_ref[...], preferred_element_type=_F32)
        u = jnp.dot(a, w3_ref[...], preferred_element_type=_F32)
        o_ref[...] = (g * jax.nn.sigmoid(g) * u).astype(o_ref.dtype)

    @pl.when(pl.program_id(1) >= nv_ref[0])
    def _():
        o_ref[...] = jnp.zeros_like(o_ref)


def _moe_up(tid, te, nv, xs, w1, w3, tm, tn=512):
    p, k = xs.shape
    n = w1.shape[2]
    tn = _tile(n, tn, _LANES)
    wspec = pl.BlockSpec((None, k, tn), lambda j, i, tid, te, nv: (te[i], 0, j))
    return pl.pallas_call(
        _moe_up_kernel,
        grid_spec=pltpu.PrefetchScalarGridSpec(
            num_scalar_prefetch=3,
            grid=(n // tn, p // tm),
            in_specs=[pl.BlockSpec((tm, k), lambda j, i, tid, te, nv: (tid[i], 0)), wspec, wspec],
            out_specs=pl.BlockSpec((tm, tn), lambda j, i, tid, te, nv: (i, j))),
        out_shape=jax.ShapeDtypeStruct((p, n), _CDT),
        compiler_params=_cp("arbitrary", "arbitrary"),
        name="moe_swiglu_up",
    )(tid, te, nv, xs, w1, w3)


def _moe_down_kernel(tid_ref, te_ref, nv_ref, a_ref, w_ref, o_ref):
    @pl.when(pl.program_id(0) < nv_ref[0])
    def _():
        d = jnp.dot(a_ref[...], w_ref[...], preferred_element_type=_F32)

        @pl.when(pl.program_id(1) == 0)
        def _():
            o_ref[...] = d

        @pl.when(pl.program_id(1) != 0)
        def _():
            o_ref[...] += d

    @pl.when((pl.program_id(0) >= nv_ref[0]) & (pl.program_id(1) == 0))
    def _():
        o_ref[...] = jnp.zeros_like(o_ref)


def _moe_down(tid, te, nv, hmid, w2, tm, tk=1024):
    p, k = hmid.shape
    n = w2.shape[2]
    tk = _tile(k, tk, _LANES)
    nk = k // tk

    def kk(i, c, nv):
        return jnp.where(i < nv[0], c, nk - 1)

    return pl.pallas_call(
        _moe_down_kernel,
        grid_spec=pltpu.PrefetchScalarGridSpec(
            num_scalar_prefetch=3,
            grid=(p // tm, nk),
            in_specs=[pl.BlockSpec((tm, tk), lambda i, c, tid, te, nv: (tid[i], kk(i, c, nv))),
                      pl.BlockSpec((None, tk, n), lambda i, c, tid, te, nv: (te[i], kk(i, c, nv), 0))],
            out_specs=pl.BlockSpec((tm, n), lambda i, c, tid, te, nv: (i, 0))),
        out_shape=jax.ShapeDtypeStruct((p, n), _F32),
        compiler_params=_cp("arbitrary", "arbitrary"),
        name="moe_down",
    )(tid, te, nv, hmid, w2)


def _combine_kernel(pos_ref, ys_hbm, h_ref, info_ref, g_ref, o_ref, on_ref, buf, sem, *, tg):
    base = pl.program_id(0) * tg

    def start(r, c):
        for s in range(2):
            _row_copy(ys_hbm, pos_ref[2 * (base + r) + s], buf.at[s, pl.ds(r, 1)], sem).start()
        return c

    lax.fori_loop(0, tg, start, 0)

    def wait(r, c):
        for s in range(2):
            _row_copy(ys_hbm, 0, buf.at[s, pl.ds(r, 1)], sem).wait()
        return c

    lax.fori_loop(0, tg, wait, 0)
    info = info_ref[...]
    out = h_ref[...] + info[:, 2:3] * buf[0] + info[:, 3:4] * buf[1]
    o_ref[...] = out
    on_ref[...] = _rms(out, g_ref[...]).astype(on_ref.dtype)


def _combine(pos, ys, h, info, g, tg=128):
    m, d = h.shape
    tg = _tile(m, tg)
    row = lambda i, s: (i, 0)
    return pl.pallas_call(
        functools.partial(_combine_kernel, tg=tg),
        grid_spec=pltpu.PrefetchScalarGridSpec(
            num_scalar_prefetch=1,
            grid=(m // tg,),
            in_specs=[pl.BlockSpec(memory_space=pl.ANY), pl.BlockSpec((tg, d), row),
                      pl.BlockSpec((tg, _LANES), row), pl.BlockSpec((1, d), lambda i, s: (0, 0))],
            out_specs=[pl.BlockSpec((tg, d), row), pl.BlockSpec((tg, d), row)],
            scratch_shapes=[pltpu.VMEM((2, tg, d), _F32), pltpu.SemaphoreType.DMA]),
        out_shape=[jax.ShapeDtypeStruct((m, d), _F32), jax.ShapeDtypeStruct((m, d), _CDT)],
        compiler_params=_cp("arbitrary"),
        name="moe_combine",
    )(pos, ys, h, info, g.reshape(1, d))


def _moe_plan(info, tm):
    m = info.shape[0]
    e_flat = info[:, :2].astype(jnp.int32).reshape(-1)
    onehot = (e_flat[:, None] == jnp.arange(_N_EXPERTS, dtype=jnp.int32)[None, :]).astype(jnp.int32)
    csum = jnp.cumsum(onehot, axis=0)
    rank = jnp.sum((csum - onehot) * onehot, axis=1)
    counts = csum[-1]
    tiles_per = (counts + tm - 1) // tm
    tile_end = jnp.cumsum(tiles_per)
    row_start = (tile_end - tiles_per) * tm
    dest = row_start[e_flat] + rank
    n_tiles = -(-2 * m // tm) + _N_EXPERTS
    src = jnp.zeros((n_tiles * tm,), jnp.int32).at[dest].set(jnp.arange(2 * m, dtype=jnp.int32) // 2)
    n_valid = tile_end[-1]
    tid = jnp.minimum(jnp.arange(n_tiles, dtype=jnp.int32), n_valid - 1)
    te = jnp.minimum(jnp.searchsorted(tile_end, tid, side="right").astype(jnp.int32), _N_EXPERTS - 1)
    return src, dest.astype(jnp.int32), tid, te, n_valid.reshape(1).astype(jnp.int32)


def _pad_keys(past, new, lp):
    bsz, t, w = new.shape
    parts = [new] if past is None else [past.reshape(bsz, -1, w).astype(_CDT), new]
    have = sum(a.shape[1] for a in parts)
    if lp > have:
        parts.append(jnp.zeros((bsz, lp - have, w), _CDT))
    return jnp.concatenate(parts, axis=1).reshape(bsz * lp, w)


def kernel(x_prompt, x_sample, cache_a_k, cache_a_v, cache_c_k, cache_c_v, cache_c_kidx, p_prompt, p_sample, ln_mix_e, w_in_e, a_qnorm, a_knorm, a_lam, a_subln, b_vnorm, b_ws, b_bias, w_out_e, ln_ffn_e, ffn_w1, ffn_w3, ffn_w2, ln_mix_o, w_in_o, c_qnorm, c_knorm, c_kidx_norm, w_out_o, ln_ffn_o, moe_router, moe_w1, moe_w3, moe_w2, ple_proj, ple_gate, ln_ple):
    bp, tp, d = x_prompt.shape
    bs, ts, _ = x_sample.shape
    past = cache_a_k.shape[2]
    mp, ms = bp * tp, bs * ts
    m = mp + ms
    cdt = _CDT

    groups = [dict(bsz=bp, t=tp, p0=0, row0=0, caches=None),
              dict(bsz=bs, t=ts, p0=past, row0=mp, caches=(cache_a_k, cache_a_v, cache_c_k, cache_c_v, cache_c_kidx))]
    for gr in groups:
        gr["l_valid"] = gr["p0"] + gr["t"]
        gr["lp"] = -(-gr["l_valid"] // _LANES) * _LANES

    h = jnp.concatenate([x_prompt.reshape(mp, d), x_sample.reshape(ms, d)], axis=0)
    p_all = jnp.concatenate([p_prompt.reshape(2, mp, -1), p_sample.reshape(2, ms, -1)], axis=1).astype(cdt)

    lam_init0 = 0.8 - 0.6 * math.exp(-0.3 * 0)
    aw = _A_HEADS * 2 * _A_HEAD_DIM
    z = _mm(_rmsnorm(h, ln_mix_e[0]), w_in_e[0].astype(cdt))
    (qa,) = _gnorm(z, 0, aw, a_qnorm[0], [cdt])
    ka32, ka = _gnorm(z, 1, aw, a_knorm[0], [_F32, cdt])
    va32 = z[:, 2 * aw:3 * aw]
    va = va32.astype(cdt)
    oa, ob, bv = [], [], []
    for gr in groups:
        bsz, t, row0, lp = gr["bsz"], gr["t"], gr["row0"], gr["lp"]
        new_k = ka[row0:row0 + bsz * t].reshape(bsz, t, aw)
        new_v = va[row0:row0 + bsz * t].reshape(bsz, t, aw)
        ck, cv = (None, None) if gr["caches"] is None else (gr["caches"][0][0], gr["caches"][1][0])
        oa.append(_diff_attn(qa, _pad_keys(ck, new_k, lp), _pad_keys(cv, new_v, lp), a_lam[0], a_subln[0],
                             bsz=bsz, t=t, lp=lp, l_valid=gr["l_valid"], p0=gr["p0"], q_row0=row0, lam_init=lam_init0))
        tc = min(_B_CHUNK, t)
        o, v = _chunk_mlp(z, 3, 4, b_vnorm[0], b_ws[0], b_bias[0], rows=bsz * t, row0=row0, tc=tc)
        ob.append(o)
        bv.append(v)
    mix = jnp.concatenate([jnp.concatenate(oa, axis=0), jnp.concatenate(ob, axis=0)], axis=1)
    h = _mm(mix, w_out_e[0].astype(cdt), res=h)
    u = _mm_swiglu(_rmsnorm(h, ln_ffn_e[0]), ffn_w1[0].astype(cdt), ffn_w3[0].astype(cdt))
    h = _mm_kres(u, ffn_w2[0].astype(cdt), h)
    h = _ple(_rmsnorm(h, ln_ple[0]), ple_gate[0].astype(cdt), p_all[0], ple_proj[0].astype(cdt), h)

    qw = _C_HEADS * _C_HEAD_DIM
    kw = _C_KV_HEADS * _C_HEAD_DIM
    iw = _IDX_HEADS * _IDX_DIM
    main_w = 2 * kw + qw + iw
    w_o = w_in_o[0]
    tail = jnp.pad(w_o[:, main_w:], ((0, 0), (0, 2 * _LANES - (_IDX_DIM + _IDX_HEADS))))
    hn = _rmsnorm(h, ln_mix_o[0])
    zo = _mm(hn, w_o[:, :main_w].astype(cdt))
    zt = _mm(hn, tail.astype(cdt))
    (qc,) = _gnorm(zo, 0, qw, c_qnorm[0], [cdt])
    kc32, kc = _gnorm(zo, qw // kw, kw, c_knorm[0], [_F32, cdt])
    vc32 = zo[:, qw + kw:qw + 2 * kw]
    vc = vc32.astype(cdt)
    kidx32, kidx = _gnorm(zt, 0, _IDX_DIM, c_kidx_norm[0], [_F32, cdt])
    oc = []
    for gr in groups:
        bsz, t, row0, lp = gr["bsz"], gr["t"], gr["row0"], gr["lp"]
        rows = slice(row0, row0 + bsz * t)
        c = gr["caches"]
        ck, cv, cki = (None, None, None) if c is None else (c[2][0], c[3][0], c[4][0])
        k_all = _pad_keys(ck, kc[rows].reshape(bsz, t, kw), lp)
        v_all = _pad_keys(cv, vc[rows].reshape(bsz, t, kw), lp)
        ki_all = _pad_keys(cki, kidx[rows].reshape(bsz, t, _IDX_DIM), lp)
        topk = min(_TOPK_MAX, gr["l_valid"] // 4)
        tri = (jnp.arange(lp)[:, None] < jnp.arange(lp)[None, :]).astype(cdt)
        mask = _dsa_index(zo, (qw + 2 * kw) // (8 * _IDX_DIM), zt, 1, ki_all, tri, bsz=bsz, t=t, lp=lp,
                          l_valid=gr["l_valid"], p0=gr["p0"], q_row0=row0, topk=topk)
        oc.append(_dsa_attn(qc, k_all, v_all, mask, bsz=bsz, t=t, lp=lp, q_row0=row0))
    h = _mm(jnp.concatenate(oc, axis=0), w_out_o[0].astype(cdt), res=h)

    router_w = jnp.pad(moe_router[0], ((0, 0), (0, _LANES - _N_EXPERTS))).astype(cdt)
    info = _router(h, ln_ffn_o[0], router_w)
    tm = _tile(2 * m, _MOE_TM)
    src, pos, tid, te, nv = _moe_plan(info, tm)
    xs = _gather_norm(src, h, ln_ffn_o[0])
    hmid = _moe_up(tid, te, nv, xs, moe_w1[0].astype(cdt), moe_w3[0].astype(cdt), tm)
    ys = _moe_down(tid, te, nv, hmid, moe_w2[0].astype(cdt), tm)
    h, hn = _combine(pos, ys, h, info, ln_ple[1])
    h = _ple(hn, ple_gate[1].astype(cdt), p_all[1], ple_proj[1].astype(cdt), h)

    def split(a, *tail_shape):
        return (a[:mp].reshape(1, bp, tp, *tail_shape), a[mp:].reshape(1, bs, ts, *tail_shape))

    ak_p, ak_s = split(ka32, _A_HEADS, 2, _A_HEAD_DIM)
    av_p, av_s = split(va32, _A_HEADS, 2 * _A_HEAD_DIM)
    ck_p, ck_s = split(kc32, _C_KV_HEADS, _C_HEAD_DIM)
    cv_p, cv_s = split(vc32, _C_KV_HEADS, _C_HEAD_DIM)
    cki_p, cki_s = split(kidx32, _IDX_DIM)
    bv_s = bv[1].reshape(1, bs, ts, _B_GROUPS, _B_GROUP_DIM)
    return (h[:mp].reshape(bp, tp, d), h[mp:].reshape(bs, ts, d),
            ak_p, av_p, ck_p, cv_p, cki_p, ak_s, av_s, bv_s, ck_s, cv_s, cki_s)
```

```python
import functools
import math

import numpy as np
import jax
import jax.numpy as jnp
from jax import lax
from jax.experimental import pallas as pl
from jax.experimental.pallas import tpu as pltpu

_CDT = jnp.bfloat16
_F32 = jnp.float32
_EPS = 1e-6
_LANES = 128
_VMEM_LIMIT = 56 * 1024 * 1024

_CHUNK = 64
_A_HEADS = 8
_A_HEAD_DIM = 128
_B_GROUPS = 8
_B_GROUP_DIM = 256
_B_CHUNK = 128
_C_HEADS = 32
_C_KV_HEADS = 4
_C_HEAD_DIM = 128
_IDX_HEADS = 32
_IDX_DIM = 128
_TOPK_MAX = 256
_N_EXPERTS = 8
_MOE_TM = 512

_NT = (((1,), (1,)), ((), ()))
_LOG2E = math.log2(math.e)


def _tile(n, pref, align=8):
    if n <= pref:
        return n
    for t in range(pref - pref % align, 0, -align):
        if n % t == 0:
            return t
    raise ValueError(f"no tile for {n} <= {pref}")


def _cp(*sem, vmem=_VMEM_LIMIT):
    return pltpu.CompilerParams(dimension_semantics=sem, vmem_limit_bytes=vmem)


def _rms(x, g):
    return x * lax.rsqrt(jnp.mean(x * x, axis=-1, keepdims=True) + _EPS) * g


def _rmsnorm_kernel(x_ref, g_ref, o_ref):
    o_ref[...] = _rms(x_ref[...], g_ref[...]).astype(o_ref.dtype)


def _rmsnorm(x, g):
    m, d = x.shape
    tm = _tile(m, 256)
    return pl.pallas_call(
        _rmsnorm_kernel,
        grid=(m // tm,),
        in_specs=[pl.BlockSpec((tm, d), lambda i: (i, 0)), pl.BlockSpec((1, d), lambda i: (0, 0))],
        out_specs=pl.BlockSpec((tm, d), lambda i: (i, 0)),
        out_shape=jax.ShapeDtypeStruct((m, d), _CDT),
        compiler_params=_cp("parallel"),
        name="rmsnorm",
    )(x, g.reshape(1, d))


def _gnorm_kernel(z_ref, g_ref, *o_refs, groups, scale):
    g = g_ref[...]
    for i in range(groups):
        sl = slice(i * _LANES, (i + 1) * _LANES)
        y = _rms(z_ref[:, sl], g)
        if scale != 1.0:
            y = y * scale
        for o in o_refs:
            o[:, sl] = y.astype(o.dtype)


def _gnorm(z, col_block, width, g, out_dtypes, scale=1.0):
    m = z.shape[0]
    tm = _tile(m, 512)
    outs = pl.pallas_call(
        functools.partial(_gnorm_kernel, groups=width // _LANES, scale=scale),
        grid=(m // tm,),
        in_specs=[pl.BlockSpec((tm, width), lambda i: (i, col_block)),
                  pl.BlockSpec((1, _LANES), lambda i: (0, 0))],
        out_specs=[pl.BlockSpec((tm, width), lambda i: (i, 0)) for _ in out_dtypes],
        out_shape=[jax.ShapeDtypeStruct((m, width), dt) for dt in out_dtypes],
        compiler_params=_cp("parallel"),
        name="head_rmsnorm",
    )(z, g.reshape(1, _LANES))
    return outs


def _mm_kernel(a_ref, b_ref, o_ref):
    o_ref[...] = jnp.dot(a_ref[...], b_ref[...], preferred_element_type=_F32).astype(o_ref.dtype)


def _mm_res_kernel(a_ref, b_ref, r_ref, o_ref):
    o_ref[...] = r_ref[...] + jnp.dot(a_ref[...], b_ref[...], preferred_element_type=_F32)


def _mm(a, b, res=None, tm=1024, tn=1024):
    m, k = a.shape
    n = b.shape[1]
    tm, tn = _tile(m, tm), _tile(n, tn, _LANES)
    in_specs = [pl.BlockSpec((tm, k), lambda j, i: (i, 0)), pl.BlockSpec((k, tn), lambda j, i: (0, j))]
    args = [a, b]
    if res is not None:
        in_specs.append(pl.BlockSpec((tm, tn), lambda j, i: (i, j)))
        args.append(res)
    return pl.pallas_call(
        _mm_kernel if res is None else _mm_res_kernel,
        grid=(n // tn, m // tm),
        in_specs=in_specs,
        out_specs=pl.BlockSpec((tm, tn), lambda j, i: (i, j)),
        out_shape=jax.ShapeDtypeStruct((m, n), _F32),
        compiler_params=_cp("parallel", "parallel"),
        name="matmul",
    )(*args)


def _swiglu_kernel(a_ref, w1_ref, w3_ref, o_ref):
    a = a_ref[...]
    g = jnp.dot(a, w1_ref[...], preferred_element_type=_F32)
    u = jnp.dot(a, w3_ref[...], preferred_element_type=_F32)
    o_ref[...] = (g * jax.nn.sigmoid(g) * u).astype(o_ref.dtype)


def _mm_swiglu(a, w1, w3, tm=1024, tn=512):
    m, k = a.shape
    n = w1.shape[1]
    tm, tn = _tile(m, tm), _tile(n, tn, _LANES)
    wspec = pl.BlockSpec((k, tn), lambda j, i: (0, j))
    return pl.pallas_call(
        _swiglu_kernel,
        grid=(n // tn, m // tm),
        in_specs=[pl.BlockSpec((tm, k), lambda j, i: (i, 0)), wspec, wspec],
        out_specs=pl.BlockSpec((tm, tn), lambda j, i: (i, j)),
        out_shape=jax.ShapeDtypeStruct((m, n), _CDT),
        compiler_params=_cp("parallel", "parallel"),
        name="swiglu_up",
    )(a, w1, w3)


def _mm_kres_kernel(a_ref, b_ref, r_ref, o_ref):
    d = jnp.dot(a_ref[...], b_ref[...], preferred_element_type=_F32)

    @pl.when(pl.program_id(2) == 0)
    def _():
        o_ref[...] = r_ref[...] + d

    @pl.when(pl.program_id(2) != 0)
    def _():
        o_ref[...] += d


def _mm_kres(a, b, res, tm=1024, tn=1024, tk=3584):
    m, k = a.shape
    n = b.shape[1]
    tm, tn, tk = _tile(m, tm), _tile(n, tn, _LANES), _tile(k, tk, _LANES)
    return pl.pallas_call(
        _mm_kres_kernel,
        grid=(n // tn, m // tm, k // tk),
        in_specs=[pl.BlockSpec((tm, tk), lambda j, i, kk: (i, kk)),
                  pl.BlockSpec((tk, tn), lambda j, i, kk: (kk, j)),
                  pl.BlockSpec((tm, tn), lambda j, i, kk: (i, j))],
        out_specs=pl.BlockSpec((tm, tn), lambda j, i, kk: (i, j)),
        out_shape=jax.ShapeDtypeStruct((m, n), _F32),
        compiler_params=_cp("parallel", "parallel", "arbitrary"),
        name="matmul_ksplit",
    )(a, b, res)


def _ple_kernel(hn_ref, gw_ref, p_ref, pw_ref, h_ref, o_ref):
    gate = jnp.dot(hn_ref[...], gw_ref[...], preferred_element_type=_F32)
    emb = jnp.dot(p_ref[...], pw_ref[...], preferred_element_type=_F32)
    o_ref[...] = h_ref[...] + emb * jax.nn.sigmoid(gate)


def _ple(hn, gate_w, p, proj_w, h, rows=None, tm=1024, tn=512):
    k = hn.shape[1]
    n = gate_w.shape[1]
    pd = p.shape[1]
    row0, m = (0, hn.shape[0]) if rows is None else rows
    tm, tn = _tile(math.gcd(m, row0) if row0 else m, tm), _tile(n, tn, _LANES)
    rb0 = row0 // tm
    return pl.pallas_call(
        _ple_kernel,
        grid=(n // tn, m // tm),
        in_specs=[pl.BlockSpec((tm, k), lambda j, i: (rb0 + i, 0)),
                  pl.BlockSpec((k, tn), lambda j, i: (0, j)),
                  pl.BlockSpec((tm, pd), lambda j, i: (rb0 + i, 0)),
                  pl.BlockSpec((pd, tn), lambda j, i: (0, j)),
                  pl.BlockSpec((tm, tn), lambda j, i: (rb0 + i, j))],
        out_specs=pl.BlockSpec((tm, tn), lambda j, i: (i, j)),
        out_shape=jax.ShapeDtypeStruct((m, n), _F32),
        compiler_params=_cp("parallel", "parallel"),
        name="ple_gate",
    )(hn, gate_w, p, proj_w, h)


def _num_key_ranges(lp, p0, l_valid):
    if p0 != 0 or l_valid != lp:
        return 1
    for n in (8, 4, 2):
        if lp % (n * _LANES) == 0:
            return n
    return 1


def _for_key_range(i, tq, lp, n_ranges, body):
    if n_ranges == 1:
        body(lp)
        return
    seg = lp // n_ranges
    which = ((i + 1) * tq - 1) // seg
    for c in range(n_ranges):
        pl.when(which == c)(functools.partial(body, (c + 1) * seg))


def _visible(i, tq, kv, p0, l_valid, lp):
    qpos = p0 + i * tq + lax.broadcasted_iota(jnp.int32, (tq, kv), 0)
    kpos = lax.broadcasted_iota(jnp.int32, (tq, kv), 1)
    vis = (kpos >> 6) <= (qpos >> 6)
    return vis if l_valid == lp else vis & (kpos < l_valid)


def _diff_attn_kernel(q_ref, k_ref, v_ref, lam_ref, sub_ref, o_ref, *, tq, lp, l_valid, p0, lam_init, n_ranges):
    i = pl.program_id(2)
    lq = lam_ref[...]
    lam = (jnp.exp(jnp.sum(lq[0:1] * lq[1:2], axis=-1, keepdims=True))
           - jnp.exp(jnp.sum(lq[2:3] * lq[3:4], axis=-1, keepdims=True)) + lam_init)

    def body(kv):
        visible = _visible(i, tq, kv, p0, l_valid, lp)
        probs = []
        for c in range(2):
            sl = slice(c * _A_HEAD_DIM, (c + 1) * _A_HEAD_DIM)
            s = lax.dot_general(q_ref[:, sl], k_ref[0:kv, sl], _NT, preferred_element_type=_F32)
            s = jnp.where(visible, s, -jnp.inf)
            e = jnp.exp2(s - jnp.max(s, axis=-1, keepdims=True))
            probs.append(e * (1.0 / jnp.sum(e, axis=-1, keepdims=True)))
        a = (probs[0] - lam * probs[1]).astype(_CDT)
        o = jnp.dot(a, v_ref[0:kv, :], preferred_element_type=_F32)
        o_ref[...] = (_rms(o, sub_ref[...]) * (1.0 - lam_init)).astype(o_ref.dtype)

    _for_key_range(i, tq, lp, n_ranges, body)


def _diff_attn(q, k, v, lam, subln, *, bsz, t, lp, l_valid, p0, q_row0, lam_init):
    hw = 2 * _A_HEAD_DIM
    tq = _tile(t, 256)
    nq = t // tq
    qb0 = q_row0 // tq
    kern = functools.partial(_diff_attn_kernel, tq=tq, lp=lp, l_valid=l_valid, p0=p0, lam_init=lam_init,
                             n_ranges=_num_key_ranges(lp, p0, l_valid))
    return pl.pallas_call(
        kern,
        grid=(bsz, _A_HEADS, nq),
        in_specs=[pl.BlockSpec((tq, hw), lambda b, h, i: (qb0 + b * nq + i, h)),
                  pl.BlockSpec((lp, hw), lambda b, h, i: (b, h)),
                  pl.BlockSpec((lp, hw), lambda b, h, i: (b, h)),
                  pl.BlockSpec((4, _A_HEAD_DIM), lambda b, h, i: (0, 0)),
                  pl.BlockSpec((1, hw), lambda b, h, i: (0, 0))],
        out_specs=pl.BlockSpec((tq, hw), lambda b, h, i: (b * nq + i, h)),
        out_shape=jax.ShapeDtypeStruct((bsz * t, _A_HEADS * hw), _CDT),
        compiler_params=_cp("parallel", "parallel", "parallel"),
        name="diff_attention",
    )(q, k, v, lam, subln.reshape(1, hw))


def _chunk_mlp_kernel(u_ref, v_ref, g_ref, ws_ref, b_ref, o_ref, vo_ref, *, tc):
    row = lax.broadcasted_iota(jnp.int32, (tc, tc), 0)
    col = lax.broadcasted_iota(jnp.int32, (tc, tc), 1)
    for g in range(_B_GROUPS):
        sl = slice(g * _B_GROUP_DIM, (g + 1) * _B_GROUP_DIM)
        u = jax.nn.gelu(u_ref[:, sl])
        v = _rms(jax.nn.gelu(v_ref[:, sl]), g_ref[g])
        vo_ref[:, sl] = v
        w = jnp.where(row >= col, ws_ref[g], 0.0).astype(_CDT)
        s = jnp.dot(w, v.astype(_CDT), preferred_element_type=_F32) + b_ref[g]
        o_ref[:, sl] = (u * s).astype(o_ref.dtype)


def _chunk_mlp(z, u_col, v_col, vnorm, ws, bias, *, rows, row0, tc):
    bw = _B_GROUPS * _B_GROUP_DIM
    rb0 = row0 // tc
    return pl.pallas_call(
        functools.partial(_chunk_mlp_kernel, tc=tc),
        grid=(rows // tc,),
        in_specs=[pl.BlockSpec((tc, bw), lambda i: (rb0 + i, u_col)),
                  pl.BlockSpec((tc, bw), lambda i: (rb0 + i, v_col)),
                  pl.BlockSpec((_B_GROUPS, 1, _B_GROUP_DIM), lambda i: (0, 0, 0)),
                  pl.BlockSpec((_B_GROUPS, tc, tc), lambda i: (0, 0, 0)),
                  pl.BlockSpec((_B_GROUPS, tc, 1), lambda i: (0, 0, 0))],
        out_specs=[pl.BlockSpec((tc, bw), lambda i: (i, 0)), pl.BlockSpec((tc, bw), lambda i: (i, 0))],
        out_shape=[jax.ShapeDtypeStruct((rows, bw), _CDT), jax.ShapeDtypeStruct((rows, bw), _F32)],
        compiler_params=_cp("parallel"),
        name="chunk_mlp",
    )(z, z, vnorm.reshape(_B_GROUPS, 1, _B_GROUP_DIM), ws[:, :tc, :tc], bias[:, :tc].reshape(_B_GROUPS, tc, 1))


def _sortable_key(x):
    bits = lax.bitcast_convert_type(x, jnp.int32)
    return jnp.where(bits >= 0, bits, bits ^ jnp.int32(0x7FFFFFFF))


_KEY_NEG_INF = int(np.array(-np.inf, np.float32).view(np.int32)) ^ 0x7FFFFFFF
_INT_MIN = -(1 << 31)


def _dsa_index_kernel(qi_ref, wi_ref, ki_ref, tri_ref, mask_ref, acc_ref, *, tq, lp, l_valid, p0, topk, heads_per_step, n_ranges):
    i = pl.program_id(1)
    s = pl.program_id(2)
    w = wi_ref[...]
    lane = lax.broadcasted_iota(jnp.int32, w.shape, 1)
    w = w * (_IDX_DIM ** -0.5 * _IDX_HEADS ** -0.5)

    def body(kv):
        ki = ki_ref[0:kv, :]

        @pl.when(s == 0)
        def _():
            acc_ref[:, 0:kv] = jnp.zeros((tq, kv), _F32)

        acc = acc_ref[:, 0:kv]
        for r in range(heads_per_step):
            qh = qi_ref[:, r * _IDX_DIM:(r + 1) * _IDX_DIM].astype(_CDT)
            d = lax.dot_general(qh, ki, _NT, preferred_element_type=_F32)
            wcol = jnp.sum(jnp.where(lane == s * heads_per_step + r, w, 0.0), axis=-1, keepdims=True)
            acc = acc + jnp.maximum(d, 0.0) * wcol
        acc_ref[:, 0:kv] = acc

        @pl.when(s == pl.num_programs(2) - 1)
        def _():
            admissible = _visible(i, tq, kv, p0, l_valid, lp)
            score = jnp.where(acc == 0.0, 0.0, acc)
            key = _sortable_key(jnp.where(admissible, score, -jnp.inf))

            def bit_step(b, t):
                cand = t | (jnp.int32(1) << (31 - b))
                cnt = jnp.sum(jnp.where(key >= (cand ^ jnp.int32(_INT_MIN)), 1.0, 0.0), axis=-1, keepdims=True)
                return jnp.where(cnt >= topk, cand, t)

            thr = lax.fori_loop(0, 32, bit_step, jnp.zeros((tq, 1), jnp.int32)) ^ jnp.int32(_INT_MIN)
            gt = key > thr
            eq = key == thr
            n_gt = jnp.sum(jnp.where(gt, 1.0, 0.0), axis=-1, keepdims=True)
            rank = jnp.dot(jnp.where(eq, 1.0, 0.0).astype(_CDT), tri_ref[0:kv, 0:kv], preferred_element_type=_F32)
            take_tie = jnp.where(rank < (topk - n_gt), 0.0, -jnp.inf)
            mask = jnp.where(gt, 0.0, jnp.where(eq, take_tie, -jnp.inf))
            mask_ref[:, 0:kv] = jnp.where(key > jnp.int32(_KEY_NEG_INF), mask, -jnp.inf)
            if kv < lp:
                mask_ref[:, kv:lp] = jnp.full((tq, lp - kv), -jnp.inf, _F32)

    _for_key_range(i, tq, lp, n_ranges, body)


def _dsa_index(zq, qi_col0, zw, wi_col, ki, tri, *, bsz, t, lp, l_valid, p0, q_row0, topk):
    tq = _tile(t, 128)
    nq = t // tq
    qb0 = q_row0 // tq
    hps = 8
    kern = functools.partial(_dsa_index_kernel, tq=tq, lp=lp, l_valid=l_valid, p0=p0, topk=topk, heads_per_step=hps,
                             n_ranges=_num_key_ranges(lp, p0, l_valid))
    return pl.pallas_call(
        kern,
        grid=(bsz, nq, _IDX_HEADS // hps),
        in_specs=[pl.BlockSpec((tq, hps * _IDX_DIM), lambda b, i, s: (qb0 + b * nq + i, qi_col0 + s)),
                  pl.BlockSpec((tq, _LANES), lambda b, i, s: (qb0 + b * nq + i, wi_col)),
                  pl.BlockSpec((lp, _IDX_DIM), lambda b, i, s: (b, 0)),
                  pl.BlockSpec((lp, lp), lambda b, i, s: (0, 0))],
        out_specs=pl.BlockSpec((tq, lp), lambda b, i, s: (b * nq + i, 0)),
        out_shape=jax.ShapeDtypeStruct((bsz * t, lp), _F32),
        scratch_shapes=[pltpu.VMEM((tq, lp), _F32)],
        compiler_params=_cp("parallel", "parallel", "arbitrary"),
        name="dsa_index_select",
    )(zq, zw, ki, tri)


def _dsa_attn_kernel(q_ref, k_ref, v_ref, mask_ref, o_ref, *, tq, lp, heads_per_kv, n_ranges):
    def body(kv):
        k = k_ref[0:kv, :]
        v = v_ref[0:kv, :]
        mask = mask_ref[:, 0:kv]
        for r in range(heads_per_kv):
            sl = slice(r * _C_HEAD_DIM, (r + 1) * _C_HEAD_DIM)
            s = lax.dot_general(q_ref[:, sl], k, _NT, preferred_element_type=_F32) + mask
            e = jnp.exp2(s - jnp.max(s, axis=-1, keepdims=True))
            o = jnp.dot(e.astype(_CDT), v, preferred_element_type=_F32)
            o_ref[:, sl] = (o * (1.0 / jnp.sum(e, axis=-1, keepdims=True))).astype(o_ref.dtype)

    _for_key_range(pl.program_id(1), tq, lp, n_ranges, body)


def _dsa_attn(q, k, v, mask, *, bsz, t, lp, l_valid, p0, q_row0):
    tq = _tile(t, 128)
    nq = t // tq
    qb0 = q_row0 // tq
    hpk = _C_HEADS // _C_KV_HEADS
    gw = hpk * _C_HEAD_DIM
    return pl.pallas_call(
        functools.partial(_dsa_attn_kernel, tq=tq, lp=lp, heads_per_kv=hpk, n_ranges=_num_key_ranges(lp, p0, l_valid)),
        grid=(bsz, nq, _C_KV_HEADS),
        in_specs=[pl.BlockSpec((tq, gw), lambda b, i, g: (qb0 + b * nq + i, g)),
                  pl.BlockSpec((lp, _C_HEAD_DIM), lambda b, i, g: (b, g)),
                  pl.BlockSpec((lp, _C_HEAD_DIM), lambda b, i, g: (b, g)),
                  pl.BlockSpec((tq, lp), lambda b, i, g: (b * nq + i, 0))],
        out_specs=pl.BlockSpec((tq, gw), lambda b, i, g: (b * nq + i, g)),
        out_shape=jax.ShapeDtypeStruct((bsz * t, _C_HEADS * _C_HEAD_DIM), _CDT),
        compiler_params=_cp("parallel", "parallel", "parallel"),
        name="dsa_attention",
    )(q, k, v, mask)


def _router_kernel(h_ref, g_ref, r_ref, o_ref):
    xn = _rms(h_ref[...], g_ref[...]).astype(_CDT)
    logits = jnp.dot(xn, r_ref[...], preferred_element_type=_F32)
    lane = lax.broadcasted_iota(jnp.int32, logits.shape, 1).astype(_F32)
    logits = jnp.where(lane < _N_EXPERTS, logits, -jnp.inf)
    m1 = jnp.max(logits, axis=-1, keepdims=True)
    i1 = jnp.min(jnp.where(logits == m1, lane, float(_LANES)), axis=-1, keepdims=True)
    rest = jnp.where(lane == i1, -jnp.inf, logits)
    m2 = jnp.max(rest, axis=-1, keepdims=True)
    i2 = jnp.min(jnp.where(rest == m2, lane, float(_LANES)), axis=-1, keepdims=True)
    e2 = jnp.exp(m2 - m1)
    w1 = 1.0 / (1.0 + e2)
    w2 = e2 / (1.0 + e2)
    o_ref[...] = jnp.where(lane == 0, i1, jnp.where(lane == 1, i2, jnp.where(lane == 2, w1, jnp.where(lane == 3, w2, 0.0))))


def _router(h, g, router_w):
    m, d = h.shape
    tm = _tile(m, 256)
    return pl.pallas_call(
        _router_kernel,
        grid=(m // tm,),
        in_specs=[pl.BlockSpec((tm, d), lambda i: (i, 0)), pl.BlockSpec((1, d), lambda i: (0, 0)),
                  pl.BlockSpec((d, _LANES), lambda i: (0, 0))],
        out_specs=pl.BlockSpec((tm, _LANES), lambda i: (i, 0)),
        out_shape=jax.ShapeDtypeStruct((m, _LANES), _F32),
        compiler_params=_cp("parallel"),
        name="moe_router",
    )(h, g.reshape(1, d), router_w)


def _row_copy(src_hbm, row, dst, sem):
    return pltpu.make_async_copy(src_hbm.at[pl.ds(row, 1)], dst, sem)


def _gather_rows(src_hbm, idx_ref, first, count, dst, sem, stride=1):
    def start(r, c):
        _row_copy(src_hbm, idx_ref[first + r * stride], dst.at[pl.ds(r, 1)], sem).start()
        return c

    lax.fori_loop(0, count, start, 0, unroll=8)


def _wait_rows(src_hbm, count, dst, sem):
    def wait(r, c):
        _row_copy(src_hbm, 0, dst.at[pl.ds(r, 1)], sem).wait()
        return c

    lax.fori_loop(0, count, wait, 0, unroll=8)


def _gather_norm_kernel(src_ref, h_hbm, g_ref, o_ref, buf, sem, *, tg):
    i = pl.program_id(0)
    slot = i % 2

    @pl.when(i == 0)
    def _():
        _gather_rows(h_hbm, src_ref, 0, tg, buf.at[0], sem.at[0])

    @pl.when(i + 1 < pl.num_programs(0))
    def _():
        _gather_rows(h_hbm, src_ref, (i + 1) * tg, tg, buf.at[1 - slot], sem.at[1 - slot])

    _wait_rows(h_hbm, tg, buf.at[slot], sem.at[slot])
    o_ref[...] = _rms(buf[slot], g_ref[...]).astype(o_ref.dtype)


def _gather_norm(src, h, g, tg=256):
    p = src.shape[0]
    d = h.shape[1]
    tg = _tile(p, tg)
    return pl.pallas_call(
        functools.partial(_gather_norm_kernel, tg=tg),
        grid_spec=pltpu.PrefetchScalarGridSpec(
            num_scalar_prefetch=1,
            grid=(p // tg,),
            in_specs=[pl.BlockSpec(memory_space=pl.ANY), pl.BlockSpec((1, d), lambda i, s: (0, 0))],
            out_specs=pl.BlockSpec((tg, d), lambda i, s: (i, 0)),
            scratch_shapes=[pltpu.VMEM((2, tg, d), _F32), pltpu.SemaphoreType.DMA((2,))]),
        out_shape=jax.ShapeDtypeStruct((p, d), _CDT),
        compiler_params=_cp("arbitrary"),
        name="moe_gather_norm",
    )(src, h, g.reshape(1, d))


def _moe_up_kernel(tid_ref, te_ref, tv_ref, a_ref, w1_ref, w3_ref, o_ref, *, tm):
    rows = tv_ref[pl.program_id(1)]
    half = tm // 2

    def swiglu(sl):
        a = a_ref[sl, :]
        g = jnp.dot(a, w1_ref[...], preferred_element_type=_F32)
        u = jnp.dot(a, w3_ref[...], preferred_element_type=_F32)
        o_ref[sl, :] = (g * jax.nn.sigmoid(g) * u).astype(o_ref.dtype)

    @pl.when(rows > half)
    def _():
        swiglu(slice(0, tm))

    @pl.when((rows > 0) & (rows <= half))
    def _():
        swiglu(slice(0, half))

    @pl.when(rows <= half)
    def _():
        o_ref[half:tm, :] = jnp.zeros((tm - half, o_ref.shape[1]), o_ref.dtype)

    @pl.when(rows == 0)
    def _():
        o_ref[0:half, :] = jnp.zeros((half, o_ref.shape[1]), o_ref.dtype)


def _moe_up(tid, te, tv, xs, w1, w3, tm, tn=1024):
    p, k = xs.shape
    n = w1.shape[2]
    tn = _tile(n, tn, _LANES)
    wspec = pl.BlockSpec((None, k, tn), lambda j, i, tid, te, tv: (te[i], 0, j))
    return pl.pallas_call(
        functools.partial(_moe_up_kernel, tm=tm),
        grid_spec=pltpu.PrefetchScalarGridSpec(
            num_scalar_prefetch=3,
            grid=(n // tn, p // tm),
            in_specs=[pl.BlockSpec((tm, k), lambda j, i, tid, te, tv: (tid[i], 0)), wspec, wspec],
            out_specs=pl.BlockSpec((tm, tn), lambda j, i, tid, te, tv: (i, j))),
        out_shape=jax.ShapeDtypeStruct((p, n), _CDT),
        compiler_params=_cp("arbitrary", "arbitrary"),
        name="moe_swiglu_up",
    )(tid, te, tv, xs, w1, w3)


def _moe_down_kernel(tid_ref, te_ref, tv_ref, a_ref, w_ref, o_ref, *, tm):
    rows = tv_ref[pl.program_id(0)]
    c = pl.program_id(2)
    half = tm // 2

    def accumulate(sl):
        d = jnp.dot(a_ref[sl, :], w_ref[...], preferred_element_type=_F32)

        @pl.when(c == 0)
        def _():
            o_ref[sl, :] = d

        @pl.when(c != 0)
        def _():
            o_ref[sl, :] += d

    @pl.when(rows > half)
    def _():
        accumulate(slice(0, tm))

    @pl.when((rows > 0) & (rows <= half))
    def _():
        accumulate(slice(0, half))

    @pl.when((rows <= half) & (c == 0))
    def _():
        o_ref[half:tm, :] = jnp.zeros((tm - half, o_ref.shape[1]), o_ref.dtype)

    @pl.when((rows == 0) & (c == 0))
    def _():
        o_ref[0:half, :] = jnp.zeros((half, o_ref.shape[1]), o_ref.dtype)


def _moe_down(tid, te, tv, hmid, w2, tm, tn=2048, tk=2048):
    p, k = hmid.shape
    n = w2.shape[2]
    tn, tk = _tile(n, tn, _LANES), _tile(k, tk, _LANES)
    nk = k // tk

    def kk(i, c, tv):
        return jnp.where(tv[i] > 0, c, nk - 1)

    return pl.pallas_call(
        functools.partial(_moe_down_kernel, tm=tm),
        grid_spec=pltpu.PrefetchScalarGridSpec(
            num_scalar_prefetch=3,
            grid=(p // tm, n // tn, nk),
            in_specs=[pl.BlockSpec((tm, tk), lambda i, j, c, tid, te, tv: (tid[i], kk(i, c, tv))),
                      pl.BlockSpec((None, tk, tn), lambda i, j, c, tid, te, tv: (te[i], kk(i, c, tv), j))],
            out_specs=pl.BlockSpec((tm, tn), lambda i, j, c, tid, te, tv: (i, j))),
        out_shape=jax.ShapeDtypeStruct((p, n), _F32),
        compiler_params=_cp("arbitrary", "arbitrary", "arbitrary"),
        name="moe_down",
    )(tid, te, tv, hmid, w2)


def _combine_kernel(pos_ref, ys_hbm, h_ref, info_ref, g_ref, o_ref, on_ref, buf, sem, *, tg):
    i = pl.program_id(0)
    slot = i % 2

    def gather(tile, into):
        for c in range(2):
            _gather_rows(ys_hbm, pos_ref, 2 * tile * tg + c, tg, buf.at[into, c], sem.at[into], stride=2)

    @pl.when(i == 0)
    def _():
        gather(0, 0)

    @pl.when(i + 1 < pl.num_programs(0))
    def _():
        gather(i + 1, 1 - slot)

    for c in range(2):
        _wait_rows(ys_hbm, tg, buf.at[slot, c], sem.at[slot])
    info = info_ref[...]
    out = h_ref[...] + info[:, 2:3] * buf[slot, 0] + info[:, 3:4] * buf[slot, 1]
    o_ref[...] = out
    on_ref[...] = _rms(out, g_ref[...]).astype(on_ref.dtype)


def _combine(pos, ys, h, info, g, tg=128):
    m, d = h.shape
    tg = _tile(m, tg)
    row = lambda i, s: (i, 0)
    return pl.pallas_call(
        functools.partial(_combine_kernel, tg=tg),
        grid_spec=pltpu.PrefetchScalarGridSpec(
            num_scalar_prefetch=1,
            grid=(m // tg,),
            in_specs=[pl.BlockSpec(memory_space=pl.ANY), pl.BlockSpec((tg, d), row),
                      pl.BlockSpec((tg, _LANES), row), pl.BlockSpec((1, d), lambda i, s: (0, 0))],
            out_specs=[pl.BlockSpec((tg, d), row), pl.BlockSpec((tg, d), row)],
            scratch_shapes=[pltpu.VMEM((2, 2, tg, d), _F32), pltpu.SemaphoreType.DMA((2,))]),
        out_shape=[jax.ShapeDtypeStruct((m, d), _F32), jax.ShapeDtypeStruct((m, d), _CDT)],
        compiler_params=_cp("arbitrary"),
        name="moe_combine",
    )(pos, ys, h, info, g.reshape(1, d))


def _moe_plan(info, tm):
    m = info.shape[0]
    e_flat = info[:, :2].astype(jnp.int32).reshape(-1)
    onehot = (e_flat[:, None] == jnp.arange(_N_EXPERTS, dtype=jnp.int32)[None, :]).astype(jnp.int32)
    csum = jnp.cumsum(onehot, axis=0)
    rank = jnp.sum((csum - onehot) * onehot, axis=1)
    counts = csum[-1]
    tiles_per = (counts + tm - 1) // tm
    tile_end = jnp.cumsum(tiles_per)
    row_start = (tile_end - tiles_per) * tm
    dest = row_start[e_flat] + rank
    n_tiles = -(-2 * m // tm) + _N_EXPERTS
    src = jnp.zeros((n_tiles * tm,), jnp.int32).at[dest].set(jnp.arange(2 * m, dtype=jnp.int32) // 2)
    n_valid = tile_end[-1]
    tiles = jnp.arange(n_tiles, dtype=jnp.int32)
    tid = jnp.minimum(tiles, n_valid - 1)
    te = jnp.minimum(jnp.searchsorted(tile_end, tid, side="right").astype(jnp.int32), _N_EXPERTS - 1)
    left = counts[te] - (tid * tm - row_start[te])
    tv = jnp.where(tiles < n_valid, jnp.clip(left, 0, tm), 0).astype(jnp.int32)
    return src, dest.astype(jnp.int32), tid, te, tv


def _pad_keys(past, new, lp):
    bsz, t, w = new.shape
    if past is None and lp == t:
        return new.reshape(bsz * t, w)
    parts = [new] if past is None else [past.reshape(bsz, -1, w).astype(_CDT), new]
    have = sum(a.shape[1] for a in parts)
    if lp > have:
        parts.append(jnp.zeros((bsz, lp - have, w), _CDT))
    return jnp.concatenate(parts, axis=1).reshape(bsz * lp, w)


def kernel(x_prompt, x_sample, cache_a_k, cache_a_v, cache_c_k, cache_c_v, cache_c_kidx, p_prompt, p_sample, ln_mix_e, w_in_e, a_qnorm, a_knorm, a_lam, a_subln, b_vnorm, b_ws, b_bias, w_out_e, ln_ffn_e, ffn_w1, ffn_w3, ffn_w2, ln_mix_o, w_in_o, c_qnorm, c_knorm, c_kidx_norm, w_out_o, ln_ffn_o, moe_router, moe_w1, moe_w3, moe_w2, ple_proj, ple_gate, ln_ple):
    bp, tp, d = x_prompt.shape
    bs, ts, _ = x_sample.shape
    past = cache_a_k.shape[2]
    mp, ms = bp * tp, bs * ts
    m = mp + ms
    cdt = _CDT

    groups = [dict(bsz=bp, t=tp, p0=0, row0=0, caches=None),
              dict(bsz=bs, t=ts, p0=past, row0=mp, caches=(cache_a_k, cache_a_v, cache_c_k, cache_c_v, cache_c_kidx))]
    for gr in groups:
        gr["l_valid"] = gr["p0"] + gr["t"]
        gr["lp"] = -(-gr["l_valid"] // _LANES) * _LANES

    h = jnp.concatenate([x_prompt.reshape(mp, d), x_sample.reshape(ms, d)], axis=0)
    p_all = jnp.concatenate([p_prompt.reshape(2, mp, -1), p_sample.reshape(2, ms, -1)], axis=1).astype(cdt)

    lam_init0 = 0.8 - 0.6 * math.exp(-0.3 * 0)
    aw = _A_HEADS * 2 * _A_HEAD_DIM
    z = _mm(_rmsnorm(h, ln_mix_e[0]), w_in_e[0].astype(cdt))
    (qa,) = _gnorm(z, 0, aw, a_qnorm[0], [cdt], scale=_A_HEAD_DIM ** -0.5 * _LOG2E)
    ka32, ka = _gnorm(z, 1, aw, a_knorm[0], [_F32, cdt])
    va32 = z[:, 2 * aw:3 * aw]
    va = va32.astype(cdt)
    oa, ob, bv = [], [], []
    for gr in groups:
        bsz, t, row0, lp = gr["bsz"], gr["t"], gr["row0"], gr["lp"]
        new_k = ka[row0:row0 + bsz * t].reshape(bsz, t, aw)
        new_v = va[row0:row0 + bsz * t].reshape(bsz, t, aw)
        ck, cv = (None, None) if gr["caches"] is None else (gr["caches"][0][0], gr["caches"][1][0])
        oa.append(_diff_attn(qa, _pad_keys(ck, new_k, lp), _pad_keys(cv, new_v, lp), a_lam[0], a_subln[0],
                             bsz=bsz, t=t, lp=lp, l_valid=gr["l_valid"], p0=gr["p0"], q_row0=row0, lam_init=lam_init0))
        tc = min(_B_CHUNK, t)
        o, v = _chunk_mlp(z, 3, 4, b_vnorm[0], b_ws[0], b_bias[0], rows=bsz * t, row0=row0, tc=tc)
        ob.append(o)
        bv.append(v)
    mix = jnp.concatenate([jnp.concatenate(oa, axis=0), jnp.concatenate(ob, axis=0)], axis=1)
    h = _mm(mix, w_out_e[0].astype(cdt), res=h)
    u = _mm_swiglu(_rmsnorm(h, ln_ffn_e[0]), ffn_w1[0].astype(cdt), ffn_w3[0].astype(cdt))
    h = _mm_kres(u, ffn_w2[0].astype(cdt), h)
    h = _ple(_rmsnorm(h, ln_ple[0]), ple_gate[0].astype(cdt), p_all[0], ple_proj[0].astype(cdt), h)

    qw = _C_HEADS * _C_HEAD_DIM
    kw = _C_KV_HEADS * _C_HEAD_DIM
    iw = _IDX_HEADS * _IDX_DIM
    main_w = 2 * kw + qw + iw
    w_o = w_in_o[0]
    tail = jnp.pad(w_o[:, main_w:], ((0, 0), (0, 2 * _LANES - (_IDX_DIM + _IDX_HEADS))))
    hn = _rmsnorm(h, ln_mix_o[0])
    zo = _mm(hn, w_o[:, :main_w].astype(cdt))
    zt = _mm(hn, tail.astype(cdt))
    (qc,) = _gnorm(zo, 0, qw, c_qnorm[0], [cdt], scale=_C_HEAD_DIM ** -0.5 * _LOG2E)
    kc32, kc = _gnorm(zo, qw // kw, kw, c_knorm[0], [_F32, cdt])
    vc32 = zo[:, qw + kw:qw + 2 * kw]
    vc = vc32.astype(cdt)
    kidx32, kidx = _gnorm(zt, 0, _IDX_DIM, c_kidx_norm[0], [_F32, cdt])
    oc = []
    for gr in groups:
        bsz, t, row0, lp = gr["bsz"], gr["t"], gr["row0"], gr["lp"]
        rows = slice(row0, row0 + bsz * t)
        c = gr["caches"]
        ck, cv, cki = (None, None, None) if c is None else (c[2][0], c[3][0], c[4][0])
        k_all = _pad_keys(ck, kc[rows].reshape(bsz, t, kw), lp)
        v_all = _pad_keys(cv, vc[rows].reshape(bsz, t, kw), lp)
        ki_all = _pad_keys(cki, kidx[rows].reshape(bsz, t, _IDX_DIM), lp)
        topk = min(_TOPK_MAX, gr["l_valid"] // 4)
        tri = (jnp.arange(lp)[:, None] < jnp.arange(lp)[None, :]).astype(cdt)
        mask = _dsa_index(zo, (qw + 2 * kw) // (8 * _IDX_DIM), zt, 1, ki_all, tri, bsz=bsz, t=t, lp=lp,
                          l_valid=gr["l_valid"], p0=gr["p0"], q_row0=row0, topk=topk)
        oc.append(_dsa_attn(qc, k_all, v_all, mask, bsz=bsz, t=t, lp=lp, l_valid=gr["l_valid"], p0=gr["p0"], q_row0=row0))
    h = _mm(jnp.concatenate(oc, axis=0), w_out_o[0].astype(cdt), res=h)

    router_w = jnp.pad(moe_router[0], ((0, 0), (0, _LANES - _N_EXPERTS))).astype(cdt)
    info = _router(h, ln_ffn_o[0], router_w)
    tm = _tile(2 * m, _MOE_TM)
    src, pos, tid, te, tv = _moe_plan(info, tm)
    xs = _gather_norm(src, h, ln_ffn_o[0])
    hmid = _moe_up(tid, te, tv, xs, moe_w1[0].astype(cdt), moe_w3[0].astype(cdt), tm)
    ys = _moe_down(tid, te, tv, hmid, moe_w2[0].astype(cdt), tm)
    h, hn = _combine(pos, ys, h, info, ln_ple[1])
    y_p, y_s = (_ple(hn, ple_gate[1].astype(cdt), p_all[1], ple_proj[1].astype(cdt), h, rows=r)
                for r in ((0, mp), (mp, ms)))

    def split(a, *tail_shape):
        return (a[:mp].reshape(1, bp, tp, *tail_shape), a[mp:].reshape(1, bs, ts, *tail_shape))

    ak_p, ak_s = split(ka32, _A_HEADS, 2, _A_HEAD_DIM)
    av_p, av_s = split(va32, _A_HEADS, 2 * _A_HEAD_DIM)
    ck_p, ck_s = split(kc32, _C_KV_HEADS, _C_HEAD_DIM)
    cv_p, cv_s = split(vc32, _C_KV_HEADS, _C_HEAD_DIM)
    cki_p, cki_s = split(kidx32, _IDX_DIM)
    bv_s = bv[1].reshape(1, bs, ts, _B_GROUPS, _B_GROUP_DIM)
    return (y_p.reshape(bp, tp, d), y_s.reshape(bs, ts, d),
            ak_p, av_p, ck_p, cv_p, cki_p, ak_s, av_s, bv_s, ck_s, cv_s, cki_s)
```
